```python
import functools
import jax, jax.numpy as jnp
from jax import lax
import numpy as np

D_MODEL = 2048
BATCH = 2
SEQ = 4096
DEPTH = 1
DEC_BATCH = 8
DEC_SEQ = 4
PAST_LEN = 16384
PAGE_SIZE = 128

N_ATTN_HEADS = 8
ATTN_HEAD_DIM = 128
ATTN_WIDTH = N_ATTN_HEADS * ATTN_HEAD_DIM
MOBA_BLOCK = 256
MOBA_TOPK = 3
ATTN_Q_BLOCK = 64
N_REC_HEADS = 8
REC_KEY_DIM = 128
REC_VAL_DIM = 128
REC_KEY_WIDTH = N_REC_HEADS * REC_KEY_DIM
REC_VAL_WIDTH = N_REC_HEADS * REC_VAL_DIM
REC_CHUNK = 64
OFF_QA = 0
OFF_KA = OFF_QA + ATTN_WIDTH
OFF_VA = OFF_KA + ATTN_WIDTH
OFF_QR = OFF_VA + ATTN_WIDTH
OFF_FR = OFF_QR + REC_KEY_WIDTH
OFF_IR = OFF_FR + REC_KEY_WIDTH
OFF_OG = OFF_IR + REC_VAL_WIDTH
OFF_GA = OFF_OG + REC_VAL_WIDTH
OFF_GB = OFF_GA + D_MODEL
IN_WIDTH = OFF_GB + D_MODEL
N_GROUPS = 4
EXPERTS_PER_GROUP = 8
N_EXPERTS = N_GROUPS * EXPERTS_PER_GROUP
TOP_K = 2
D_EXPERT = 512
MOE_BLOCK = 128
LN_EPS = 1e-5
RMS_EPS = 1e-6
DEEPNORM_ALPHA = (2 * DEPTH) ** 0.25
DEEPNORM_BETA = (8 * DEPTH) ** -0.25

kernel_name = 'moba_hgrn2_hmoe_deepnorm_step'


def layer_norm(x, g, b):
    xf = x.astype(jnp.float32)
    mu = jnp.mean(xf, -1, keepdims=True)
    xc = xf - mu
    var = jnp.mean(xc * xc, -1, keepdims=True)
    return (xc * lax.rsqrt(var + LN_EPS) * g.astype(jnp.float32) + b.astype(jnp.float32)).astype(x.dtype)


def alibi_slopes():
    return 2.0 ** (-8.0 * jnp.arange(1, N_ATTN_HEADS + 1, dtype=jnp.float32) / N_ATTN_HEADS)


def moba_blocks(k, v):
    b, l, h, hd = k.shape
    nb = -(-l // MOBA_BLOCK)
    pad = nb * MOBA_BLOCK - l
    def blk(t):
        t = jnp.pad(t, ((0, 0), (0, pad), (0, 0), (0, 0)))
        return t.reshape(b, nb, MOBA_BLOCK, h, hd).transpose(0, 3, 1, 2, 4)
    kb, vb = blk(k), blk(v)
    k_mean = jnp.mean(kb.astype(jnp.float32), axis=3)
    return kb, vb, k_mean


def moba_core(q, q_pos, kb, vb, k_mean):
    b, h, nq, hd = q.shape
    nb = kb.shape[2]
    own = q_pos // MOBA_BLOCK
    gate = jnp.einsum('bhqd,bhnd->bhqn', q, k_mean)
    fully_past = jnp.arange(nb)[None, :] < own[:, None]
    gate = jnp.where(fully_past[None, None], gate, -jnp.inf)
    if nb < MOBA_TOPK:
        gate = jnp.pad(gate, ((0, 0), (0, 0), (0, 0), (0, MOBA_TOPK - nb)), constant_values=-jnp.inf)
    _, top_i = lax.top_k(gate, MOBA_TOPK)
    top_i = jnp.minimum(top_i, nb - 1)
    sel = jnp.concatenate([top_i, jnp.broadcast_to(own[None, None, :, None], (b, h, nq, 1))], -1)
    valid_blk = jnp.concatenate([jnp.arange(MOBA_TOPK)[None, :] < own[:, None],
                                 jnp.ones((nq, 1), bool)], -1)
    bi = jnp.arange(b)[:, None, None, None]
    hi = jnp.arange(h)[None, :, None, None]
    ks = kb[bi, hi, sel]
    vs = vb[bi, hi, sel]
    s = jnp.einsum('bhqd,bhqjkd->bhqjk', q.astype(ks.dtype), ks).astype(jnp.float32)
    key_pos = sel[..., None] * MOBA_BLOCK + jnp.arange(MOBA_BLOCK)
    dist = q_pos[None, None, :, None, None] - key_pos
    mask = valid_blk[None, None, :, :, None] & (dist >= 0)
    s = jnp.where(mask, s - alibi_slopes()[None, :, None, None, None] * dist.astype(jnp.float32), -jnp.inf)
    nj = sel.shape[-1]
    p = jax.nn.softmax(s.reshape(b, h, nq, nj * MOBA_BLOCK), axis=-1).reshape(s.shape)
    return jnp.einsum('bhqjk,bhqjkd->bhqd', p.astype(vs.dtype), vs).astype(jnp.float32)


def moba_prompt(q, k, v):
    b, s_len, h, hd = q.shape
    kb, vb, km = moba_blocks(k, v)
    qc = min(ATTN_Q_BLOCK, s_len)
    nc = s_len // qc
    qt = (q.astype(jnp.float32) * hd ** -0.5).reshape(b, nc, qc, h, hd).transpose(1, 0, 3, 2, 4)
    pos = jnp.arange(s_len, dtype=jnp.int32).reshape(nc, qc)
    o = lax.map(lambda a: moba_core(a[0], a[1], kb, vb, km), (qt, pos))
    return o.transpose(1, 0, 3, 2, 4).reshape(b, s_len, h, hd).astype(q.dtype)


def moba_sample(q, k, v, cache_k_l, cache_v_l, page_table):
    db, t, h, hd = q.shape
    past_k = cache_k_l[page_table].reshape(db, -1, h, hd)
    past_v = cache_v_l[page_table].reshape(db, -1, h, hd)
    past_len = past_k.shape[1]
    k_all = jnp.concatenate([past_k.astype(k.dtype), k], 1)
    v_all = jnp.concatenate([past_v.astype(v.dtype), v], 1)
    kb, vb, km = moba_blocks(k_all, v_all)
    pos = past_len + jnp.arange(t, dtype=jnp.int32)
    qt = (q.astype(jnp.float32) * hd ** -0.5).transpose(0, 2, 1, 3)
    o = moba_core(qt, pos, kb, vb, km)
    return o.transpose(0, 2, 1, 3).astype(q.dtype)


def hgrn2_recurrence(q, k, v, g, s0):
    b, t = q.shape[:2]
    c = min(REC_CHUNK, t)
    nc = -(-t // c)
    pad = nc * c - t
    def chunks(x):
        x = jnp.pad(x, ((0, 0), (0, pad), (0, 0), (0, 0)))
        return x.reshape(b, nc, c, x.shape[2], x.shape[3]).transpose(1, 0, 3, 2, 4)
    causal = jnp.tril(jnp.ones((c, c), bool))
    def step(s, inp):
        qc, kc, vc, gc = inp
        G = jnp.cumsum(gc, axis=2)
        o_inter = jnp.einsum('bhtk,bhkv->bhtv', qc * jnp.exp(G), s)
        diff = jnp.where(causal[None, None, :, :, None], G[:, :, :, None, :] - G[:, :, None, :, :], -jnp.inf)
        a = jnp.einsum('bhtk,bhsk,bhtsk->bhts', qc, kc, jnp.exp(diff))
        o_intra = jnp.einsum('bhts,bhsv->bhtv', a, vc)
        g_last = G[:, :, -1:, :]
        s_new = jnp.exp(g_last[:, :, 0, :, None]) * s + jnp.einsum('bhsk,bhsv->bhkv', kc * jnp.exp(g_last - G), vc)
        return s_new, o_inter + o_intra
    s_fin, o = lax.scan(step, s0, (chunks(q), chunks(k), chunks(v), chunks(g)))
    o = o.transpose(1, 0, 3, 2, 4).reshape(b, nc * c, q.shape[2], v.shape[3])[:, :t]
    return o, s_fin


def token_mixers(x, attend, rec_state0, w_in, lb, rec_norm_g, w_pa, w_pb, w_out):
    b, t, _ = x.shape
    f32 = jnp.float32
    proj = x @ w_in
    def heads(lo, hi, n, dh):
        return proj[..., lo:hi].reshape(b, t, n, dh)
    q = heads(OFF_QA, OFF_KA, N_ATTN_HEADS, ATTN_HEAD_DIM)
    k = heads(OFF_KA, OFF_VA, N_ATTN_HEADS, ATTN_HEAD_DIM)
    v = heads(OFF_VA, OFF_QR, N_ATTN_HEADS, ATTN_HEAD_DIM)
    o_attn = attend(q, k, v).reshape(b, t, ATTN_WIDTH)
    q_r = jax.nn.silu(heads(OFF_QR, OFF_FR, N_REC_HEADS, REC_KEY_DIM).astype(f32))
    forget = lb + (1.0 - lb) * jax.nn.sigmoid(proj[..., OFF_FR:OFF_IR].astype(f32))
    forget = forget.reshape(b, t, N_REC_HEADS, REC_KEY_DIM)
    i_r = heads(OFF_IR, OFF_OG, N_REC_HEADS, REC_VAL_DIM).astype(f32)
    o_rec, rec_state = hgrn2_recurrence(q_r, 1.0 - forget, i_r, jnp.log(forget), rec_state0.astype(f32))
    o_rec = o_rec * lax.rsqrt(jnp.mean(jnp.square(o_rec), -1, keepdims=True) + RMS_EPS) \
        * rec_norm_g.astype(f32).reshape(N_REC_HEADS, REC_VAL_DIM)
    o_rec = (o_rec * jax.nn.sigmoid(heads(OFF_OG, OFF_GA, N_REC_HEADS, REC_VAL_DIM).astype(f32)))
    o_rec = o_rec.reshape(b, t, REC_VAL_WIDTH).astype(x.dtype)
    gate_a = jax.nn.sigmoid(proj[..., OFF_GA:OFF_GB])
    gate_b = jax.nn.sigmoid(proj[..., OFF_GB:IN_WIDTH])
    merged = gate_a * (o_attn @ w_pa) + gate_b * (o_rec @ w_pb)
    return merged @ w_out, k, v, rec_state.astype(rec_state0.dtype)


def hier_moe(xt, w_group, b_group, w_router, b_router, w1, w3, w2):
    n, d = xt.shape
    f32 = jnp.float32
    g_logits = (xt @ w_group).astype(f32) + b_group.astype(f32)
    g_idx = jnp.argmax(g_logits, -1)
    g_w = jnp.take_along_axis(jax.nn.softmax(g_logits, -1), g_idx[:, None], 1)
    e_logits = jnp.einsum('nd,gde->nge', xt, w_router).astype(f32) + b_router.astype(f32)
    e_logits = jnp.take_along_axis(e_logits, g_idx[:, None, None], 1)[:, 0]
    top_v, top_i = lax.top_k(e_logits, TOP_K)
    gate = g_w * jax.nn.softmax(top_v, -1)
    expert = g_idx[:, None].astype(jnp.int32) * EXPERTS_PER_GROUP + top_i.astype(jnp.int32)
    flat_e = expert.reshape(-1)
    flat_w = gate.reshape(-1)
    nk = flat_e.shape[0]
    order = jnp.argsort(flat_e)
    e_sorted = flat_e[order]
    counts = jnp.bincount(flat_e, length=N_EXPERTS)
    padded = (counts + MOE_BLOCK - 1) // MOE_BLOCK * MOE_BLOCK
    start = jnp.cumsum(counts) - counts
    pstart = jnp.cumsum(padded) - padded
    dest = pstart[e_sorted] + jnp.arange(nk) - start[e_sorted]
    n_blk = -(-nk // MOE_BLOCK) + N_EXPERTS
    rows = n_blk * MOE_BLOCK
    row_tok = jnp.zeros((rows,), jnp.int32).at[dest].set((order // TOP_K).astype(jnp.int32))
    row_w = jnp.zeros((rows,), f32).at[dest].set(flat_w[order])
    blk_start = jnp.arange(n_blk) * MOE_BLOCK
    blk_expert = jnp.minimum(jnp.sum((pstart + padded)[None, :] <= blk_start[:, None], axis=1), N_EXPERTS - 1)
    xb = xt[row_tok].reshape(n_blk, MOE_BLOCK, d)
    def expert_block(a):
        xblk, e = a
        hid = jax.nn.silu(xblk @ w1[e]) * (xblk @ w3[e])
        return hid @ w2[e]
    yb = lax.map(expert_block, (xb, blk_expert)).reshape(rows, d)
    return jax.ops.segment_sum(yb * row_w[:, None].astype(yb.dtype), row_tok, num_segments=n)


def trunk_layer(x, attend, rec_state0, w_in, lb, rec_norm_g, w_pa, w_pb, w_out, ln1_g, ln1_b,
                w_group, b_group, w_router, b_router, w1, w3, w2, ln2_g, ln2_b):
    h, k, v, rec_state = token_mixers(x, attend, rec_state0, w_in, lb, rec_norm_g, w_pa, w_pb, w_out)
    x1 = layer_norm(DEEPNORM_ALPHA * x + h, ln1_g, ln1_b)
    m = hier_moe(x1.reshape(-1, x1.shape[-1]), w_group, b_group, w_router, b_router, w1, w3, w2).reshape(x1.shape)
    y = layer_norm(DEEPNORM_ALPHA * x1 + m, ln2_g, ln2_b)
    return y, k, v, rec_state


def setup_inputs(seed: int = 0) -> dict:
    key = jax.random.key(seed)
    ks = jax.random.split(key, 24)
    f32 = jnp.float32
    n_pages = PAST_LEN // PAGE_SIZE
    n_pool = (DEC_BATCH * n_pages * 5) // 4
    def nrm(k, shape, scale):
        return scale * jax.random.normal(k, shape, f32)
    beta = DEEPNORM_BETA
    col_scale = jnp.ones((IN_WIDTH,), f32).at[OFF_VA:OFF_QR].set(beta).at[OFF_IR:OFF_OG].set(beta)
    return {
        'x_prompt': nrm(ks[0], (BATCH, SEQ, D_MODEL), 1.0),
        'x_sample': nrm(ks[1], (DEC_BATCH, DEC_SEQ, D_MODEL), 1.0),
        'cache_k': nrm(ks[2], (DEPTH, n_pool, PAGE_SIZE, N_ATTN_HEADS, ATTN_HEAD_DIM), 1.0),
        'cache_v': nrm(ks[3], (DEPTH, n_pool, PAGE_SIZE, N_ATTN_HEADS, ATTN_HEAD_DIM), beta),
        'state_rec': nrm(ks[4], (DEPTH, DEC_BATCH, N_REC_HEADS, REC_KEY_DIM, REC_VAL_DIM), 0.5),
        'page_table': jax.random.permutation(ks[5], n_pool)[:DEC_BATCH * n_pages].reshape(DEC_BATCH, n_pages).astype(jnp.int32),
        'w_in': nrm(ks[6], (DEPTH, D_MODEL, IN_WIDTH), D_MODEL ** -0.5) * col_scale,
        'lb_logits': nrm(ks[7], (DEPTH + 1, REC_KEY_WIDTH), 0.5),
        'rec_norm_g': 1.0 + nrm(ks[8], (DEPTH, REC_VAL_WIDTH), 0.02),
        'w_pa': nrm(ks[9], (DEPTH, ATTN_WIDTH, D_MODEL), beta * ATTN_WIDTH ** -0.5),
        'w_pb': nrm(ks[10], (DEPTH, REC_VAL_WIDTH, D_MODEL), beta * REC_VAL_WIDTH ** -0.5),
        'w_out': nrm(ks[11], (DEPTH, D_MODEL, D_MODEL), beta * D_MODEL ** -0.5),
        'ln1_g': 1.0 + nrm(ks[12], (DEPTH, D_MODEL), 0.02),
        'ln1_b': nrm(ks[13], (DEPTH, D_MODEL), 0.02),
        'w_group': nrm(ks[14], (DEPTH, D_MODEL, N_GROUPS), D_MODEL ** -0.5),
        'b_group': nrm(ks[15], (DEPTH, N_GROUPS), 0.01),
        'w_router': nrm(ks[16], (DEPTH, N_GROUPS, D_MODEL, EXPERTS_PER_GROUP), D_MODEL ** -0.5),
        'b_router': nrm(ks[17], (DEPTH, N_GROUPS, EXPERTS_PER_GROUP), 0.01),
        'w1': nrm(ks[18], (DEPTH, N_EXPERTS, D_MODEL, D_EXPERT), beta * D_MODEL ** -0.5),
        'w3': nrm(ks[19], (DEPTH, N_EXPERTS, D_MODEL, D_EXPERT), beta * D_MODEL ** -0.5),
        'w2': nrm(ks[20], (DEPTH, N_EXPERTS, D_EXPERT, D_MODEL), beta * D_EXPERT ** -0.5),
        'ln2_g': 1.0 + nrm(ks[21], (DEPTH, D_MODEL), 0.02),
        'ln2_b': nrm(ks[22], (DEPTH, D_MODEL), 0.02),
    }


def reference(x_prompt, x_sample, cache_k, cache_v, state_rec, page_table, w_in, lb_logits, rec_norm_g,
              w_pa, w_pb, w_out, ln1_g, ln1_b, w_group, b_group, w_router, b_router, w1, w3, w2, ln2_g, ln2_b):
    lower_bounds = jnp.cumsum(jax.nn.softmax(lb_logits.astype(jnp.float32), axis=0), axis=0)
    b = x_prompt.shape[0]
    xp, xs = x_prompt, x_sample
    kp_l, vp_l, sp_l, ks_l, vs_l, ss_l = [], [], [], [], [], []
    for l in range(DEPTH):
        lw = (w_in[l], lower_bounds[l], rec_norm_g[l], w_pa[l], w_pb[l], w_out[l], ln1_g[l], ln1_b[l],
              w_group[l], b_group[l], w_router[l], b_router[l], w1[l], w3[l], w2[l], ln2_g[l], ln2_b[l])
        rec0 = jnp.zeros((b,) + state_rec.shape[2:], state_rec.dtype)
        xp, kp, vp, sp = trunk_layer(xp, moba_prompt, rec0, *lw)
        attend_s = functools.partial(moba_sample, cache_k_l=cache_k[l], cache_v_l=cache_v[l], page_table=page_table)
        xs, ksm, vsm, ssm = trunk_layer(xs, attend_s, state_rec[l], *lw)
        kp_l.append(kp); vp_l.append(vp); sp_l.append(sp)
        ks_l.append(ksm); vs_l.append(vsm); ss_l.append(ssm)
    return (xp, xs, jnp.stack(kp_l), jnp.stack(vp_l), jnp.stack(sp_l), jnp.stack(ks_l), jnp.stack(vs_l), jnp.stack(ss_l))
```

```python
import functools

import numpy as np
import jax
import jax.numpy as jnp
from jax import lax
from jax.experimental import pallas as pl
from jax.experimental.pallas import tpu as pltpu

F32 = jnp.float32
BF16 = jnp.bfloat16
I32 = jnp.int32

N_ATTN_HEADS = 8
ATTN_HEAD_DIM = 128
MOBA_BLOCK = 256
MOBA_TOPK = 3
N_REC_HEADS = 8
REC_DIM = 128
N_GROUPS = 4
EXPERTS_PER_GROUP = 8
N_EXPERTS = N_GROUPS * EXPERTS_PER_GROUP
TOP_K = 2
LN_EPS = 1e-5
RMS_EPS = 1e-6
NEG = -1e30

LANES = 128
SUBLANES = 8
VMEM_LIMIT = 56 * 1024 * 1024

ROW_TILE = 256
INPROJ_ROWS = 768
SEG = 1024
REC_CHUNK = 64
MOE_ROWS = 256


def _cparams(n_axes, vmem=VMEM_LIMIT):
    return pltpu.CompilerParams(dimension_semantics=("arbitrary",) * n_axes, vmem_limit_bytes=vmem)


def _sigmoid(x):
    return 1.0 / (1.0 + jnp.exp(-x))


def _dot(a, b):
    return jnp.dot(a, b, preferred_element_type=F32)


def _dot_nt(a, b):
    return lax.dot_general(a, b, (((1,), (1,)), ((), ())), preferred_element_type=F32)


def _dot_tn(a, b):
    return lax.dot_general(a, b, (((0,), (0,)), ((), ())), preferred_element_type=F32)


def _split3(x):
    hi = x.astype(BF16)
    r1 = x - hi.astype(F32)
    mid = r1.astype(BF16)
    lo = (r1 - mid.astype(F32)).astype(BF16)
    return hi, mid, lo


def _dot_nt_f32(a, b):
    a1, a2, a3 = _split3(a)
    b1, b2, b3 = _split3(b)
    out = _dot_nt(a1, b1)
    out += _dot_nt(a1, b2) + _dot_nt(a2, b1)
    out += _dot_nt(a2, b2) + _dot_nt(a1, b3) + _dot_nt(a3, b1)
    return out


def _layer_norm(z, g, b):
    mu = jnp.mean(z, axis=-1, keepdims=True)
    zc = z - mu
    var = jnp.mean(zc * zc, axis=-1, keepdims=True)
    return zc * lax.rsqrt(var + LN_EPS) * g + b


def _inproj_body(x_ref, w_ref, o_ref, wb_ref):
    @pl.when(pl.program_id(1) == 0)
    def _():
        wb_ref[...] = w_ref[...].astype(BF16)

    o_ref[...] = _dot(x_ref[...], wb_ref[...])


def _in_proj(xb, w, tm, tn):
    mp, d = xb.shape
    n = w.shape[1]
    return pl.pallas_call(
        _inproj_body,
        grid=(n // tn, mp // tm),
        in_specs=[pl.BlockSpec((tm, d), lambda j, i: (i, 0)),
                  pl.BlockSpec((d, tn), lambda j, i: (0, j))],
        out_specs=pl.BlockSpec((tm, tn), lambda j, i: (i, j)),
        out_shape=jax.ShapeDtypeStruct((mp, n), F32),
        scratch_shapes=[pltpu.VMEM((d, tn), BF16)],
        compiler_params=_cparams(2),
        name="in_proj",
    )(xb, w)


def _moba_prompt_body(slopes_ref, q_ref, k_ref, v_ref, o_ref, kb_scr, vb_scr, km_scr):
    h = pl.program_id(1)
    i = pl.program_id(2)
    nb = km_scr.shape[0]
    blk = MOBA_BLOCK

    @pl.when(i == 0)
    def _():
        kb_scr[...] = k_ref[...].astype(BF16)
        vb_scr[...] = v_ref[...].astype(BF16)
        for n in range(nb):
            km_scr[n:n + 1, :] = jnp.sum(k_ref[n * blk:(n + 1) * blk, :], axis=0, keepdims=True) * (1.0 / blk)

    q = q_ref[...] * (ATTN_HEAD_DIM ** -0.5)
    qb = q.astype(BF16)

    gate = _dot_nt_f32(q, km_scr[...])
    bidx = lax.broadcasted_iota(I32, gate.shape, 1)
    g = jnp.where(bidx < i, gate, -jnp.inf)
    sel = jnp.zeros(gate.shape, F32)
    for j in range(MOBA_TOPK):
        mx = jnp.max(g, axis=-1, keepdims=True)
        first = jnp.min(jnp.where(g == mx, bidx, nb), axis=-1, keepdims=True)
        pick = bidx == first
        sel = jnp.where(pick & (mx > -jnp.inf), 1.0, sel)
        g = jnp.where(pick, -jnp.inf, g)

    row = lax.broadcasted_iota(I32, (blk, blk), 0)
    col = lax.broadcasted_iota(I32, (blk, blk), 1)
    rel = (row - col).astype(F32)
    nslope = -slopes_ref[h]

    def block_scores(n):
        start = pl.multiple_of(n * blk, blk)
        s = _dot_nt(qb, kb_scr[pl.ds(start, blk), :])
        return s, vb_scr[pl.ds(start, blk), :]

    s, vb = block_scores(i)
    s = jnp.where(rel >= 0, s + nslope * rel, NEG)
    m0 = jnp.max(s, axis=-1, keepdims=True)
    p = jnp.exp(s - m0)
    l0 = jnp.sum(p, axis=-1, keepdims=True)
    acc0 = _dot(p.astype(BF16), vb)

    def body(n, carry):
        m, l, acc = carry
        s, vb = block_scores(n)
        shift = ((i - n) * blk).astype(F32)
        s = s + nslope * (rel + shift)
        chosen = jnp.sum(jnp.where(bidx == n, sel, 0.0), axis=-1, keepdims=True) > 0.5
        s = jnp.where(chosen, s, NEG)
        m_new = jnp.maximum(m, jnp.max(s, axis=-1, keepdims=True))
        alpha = jnp.exp(m - m_new)
        p = jnp.exp(s - m_new)
        l = alpha * l + jnp.sum(p, axis=-1, keepdims=True)
        acc = alpha * acc + _dot(p.astype(BF16), vb)
        return m_new, l, acc

    _, l, acc = lax.fori_loop(0, i, body, (m0, l0, acc0))
    o_ref[...] = (acc / l).astype(o_ref.dtype)


def _moba_prompt(proj, slopes, batch, seq, mp):
    nh, hd, blk = N_ATTN_HEADS, ATTN_HEAD_DIM, MOBA_BLOCK
    nqb = seq // blk
    return pl.pallas_call(
        _moba_prompt_body,
        grid=(batch, nh, nqb),
        in_specs=[pl.BlockSpec(memory_space=pltpu.SMEM),
                  pl.BlockSpec((blk, hd), lambda b, h, i: (b * nqb + i, h)),
                  pl.BlockSpec((seq, hd), lambda b, h, i: (b, nh + h)),
                  pl.BlockSpec((seq, hd), lambda b, h, i: (b, 2 * nh + h))],
        out_specs=pl.BlockSpec((blk, hd), lambda b, h, i: (b * nqb + i, h)),
        out_shape=jax.ShapeDtypeStruct((mp, nh * hd), BF16),
        scratch_shapes=[pltpu.VMEM((seq, hd), BF16), pltpu.VMEM((seq, hd), BF16),
                        pltpu.VMEM((nqb, hd), F32)],
        compiler_params=_cparams(3),
        name="moba_prompt",
    )(slopes, proj, proj, proj)


def _moba_decode_body(pt_ref, q_ref, bias_ref, hmask_ref, nsl_ref, k0_ref, k1_ref, v0_ref, v1_ref,
                      st_ref, o_ref, *, past_len):
    n = pl.program_id(1)
    q = q_ref[...]
    hmask = hmask_ref[...]
    base = (past_len - n * MOBA_BLOCK).astype(F32)
    lane = lax.broadcasted_iota(I32, (q.shape[0], LANES), 1)

    def half(k_ref, v_ref, bias):
        kb = k_ref[...].astype(BF16)
        s = _dot_nt(q, kb)
        gsum = jnp.sum(s * hmask, axis=-1, keepdims=True)
        logit = jnp.where(hmask > 0.5, s + bias + nsl_ref[...] * base, NEG)
        return logit, gsum, v_ref[...].astype(BF16)

    page = k0_ref.shape[0] // N_ATTN_HEADS
    lg0, g0, vb0 = half(k0_ref, v0_ref, bias_ref[0])
    lg1, g1, vb1 = half(k1_ref, v1_ref, bias_ref[1])
    m = jnp.maximum(jnp.max(lg0, axis=-1, keepdims=True), jnp.max(lg1, axis=-1, keepdims=True))
    p0 = jnp.exp(lg0 - m)
    p1 = jnp.exp(lg1 - m)
    l = jnp.sum(p0, axis=-1, keepdims=True) + jnp.sum(p1, axis=-1, keepdims=True)
    o_ref[...] = _dot(p0.astype(BF16), vb0) + _dot(p1.astype(BF16), vb1)
    gate = (g0 + g1) * (1.0 / (2 * page))
    st_ref[...] = jnp.where(lane == 0, gate, jnp.where(lane == 1, m, jnp.where(lane == 2, l, 0.0)))


def _moba_decode_stats(q_rows, cache_k2, cache_v2, page_table, bias, hmask, nsl, past_len):
    db, r, hd = q_rows.shape
    p = cache_k2.shape[1]
    n_blk = page_table.shape[1] // 2

    def page_spec(which):
        return pl.BlockSpec((None, p, hd), lambda d, n, pt: (pt[d, 2 * n + which], 0, 0))

    grid_spec = pltpu.PrefetchScalarGridSpec(
        num_scalar_prefetch=1,
        grid=(db, n_blk),
        in_specs=[pl.BlockSpec((None, r, hd), lambda d, n, pt: (d, 0, 0)),
                  pl.BlockSpec((2, r, p), lambda d, n, pt: (0, 0, 0)),
                  pl.BlockSpec((r, p), lambda d, n, pt: (0, 0)),
                  pl.BlockSpec((r, 1), lambda d, n, pt: (0, 0)),
                  page_spec(0), page_spec(1), page_spec(0), page_spec(1)],
        out_specs=[pl.BlockSpec((None, None, r, LANES), lambda d, n, pt: (d, n, 0, 0)),
                   pl.BlockSpec((None, None, r, hd), lambda d, n, pt: (d, n, 0, 0))],
    )
    return pl.pallas_call(
        functools.partial(_moba_decode_body, past_len=past_len),
        grid_spec=grid_spec,
        out_shape=[jax.ShapeDtypeStruct((db, n_blk, r, LANES), F32),
                   jax.ShapeDtypeStruct((db, n_blk, r, hd), F32)],
        compiler_params=_cparams(2),
        name="moba_decode_stats",
    )(page_table, q_rows, bias, hmask, nsl, cache_k2, cache_k2, cache_v2, cache_v2)


def _moba_decode_merge_body(st_ref, ob_ref, q_ref, kn_ref, vn_ref, nsl_ref, o_ref, *, n_new):
    st = st_ref[...]
    nb, r, _ = st.shape
    g = st[:, :, 0:1]
    m = st[:, :, 1:2]
    l = st[:, :, 2:3]
    nidx = lax.broadcasted_iota(I32, g.shape, 0)
    sel = jnp.zeros(g.shape, jnp.bool_)
    for _ in range(MOBA_TOPK):
        mx = jnp.max(g, axis=0, keepdims=True)
        first = jnp.min(jnp.where(g == mx, nidx, nb), axis=0, keepdims=True)
        pick = nidx == first
        sel = sel | pick
        g = jnp.where(pick, -jnp.inf, g)

    q = q_ref[...].astype(F32)
    t_row = (lax.broadcasted_iota(I32, (r, 1), 0) % SUBLANES).astype(F32)
    own = []
    for tp in range(n_new):
        s = jnp.sum(q * kn_ref[tp], axis=-1, keepdims=True)
        d = t_row - float(tp)
        own.append(jnp.where(d >= 0, s + nsl_ref[...] * d, NEG))

    m_sel = jnp.max(jnp.where(sel, m, NEG), axis=0)
    big = m_sel
    for s in own:
        big = jnp.maximum(big, s)
    w = jnp.exp(jnp.where(sel, m - big, NEG))
    den = jnp.sum(w * l, axis=0)
    num = jnp.sum(w * ob_ref[...], axis=0)
    for tp in range(n_new):
        e = jnp.exp(own[tp] - big)
        den = den + e
        num = num + e * vn_ref[tp]
    o_ref[...] = num / den


def _moba_decode_merge(stats, o_blk, q_rows, k_new, v_new, nsl, n_new):
    db, nb, r, hd = o_blk.shape
    return pl.pallas_call(
        functools.partial(_moba_decode_merge_body, n_new=n_new),
        grid=(db,),
        in_specs=[pl.BlockSpec((None, nb, r, LANES), lambda d: (d, 0, 0, 0)),
                  pl.BlockSpec((None, nb, r, hd), lambda d: (d, 0, 0, 0)),
                  pl.BlockSpec((None, r, hd), lambda d: (d, 0, 0)),
                  pl.BlockSpec((None, n_new, r, hd), lambda d: (d, 0, 0, 0)),
                  pl.BlockSpec((None, n_new, r, hd), lambda d: (d, 0, 0, 0)),
                  pl.BlockSpec((r, 1), lambda d: (0, 0))],
        out_specs=pl.BlockSpec((None, r, hd), lambda d: (d, 0, 0)),
        out_shape=jax.ShapeDtypeStruct((db, r, hd), F32),
        compiler_params=_cparams(1),
        name="moba_decode_merge",
    )(stats, o_blk, q_rows, k_new, v_new, nsl)


def _hgrn_levels(c):
    levels, m = [], c // 2
    while m >= 1:
        levels.append(m)
        m //= 2
    return levels


def _hgrn_consts(c):
    t = np.arange(c)
    tril = np.tril(np.ones((c, c), np.float32))
    rows, masks = [tril], []
    for m in _hgrn_levels(c):
        ref = (t // (2 * m)) * (2 * m) + m - 1
        second = (t % (2 * m)) >= m
        same = (t[:, None] // (2 * m)) == (t[None, :] // (2 * m))
        masks.append((same & second[:, None] & (~second)[None, :]).astype(np.float32))
        if m < SUBLANES:
            sel = np.zeros((c, c), np.float32)
            sel[t, ref] = 1.0
            rows.append(sel @ tril)
    return np.concatenate(rows, 0), np.stack(masks)


def _hgrn_body(qr_ref, fr_ref, ir_ref, og_ref, lbl_ref, gn_ref, s0_ref, cm_ref, mk_ref,
               o_ref, sout_ref, st_scr, *, layer, n_valid):
    c = REC_CHUNK
    t_total = qr_ref.shape[0]
    levels = _hgrn_levels(c)
    lbl = lbl_ref[...]
    e = jnp.exp(lbl - jnp.max(lbl, axis=0, keepdims=True))
    sm = e / jnp.sum(e, axis=0, keepdims=True)
    lb = jnp.sum(sm[:layer + 1], axis=0, keepdims=True)
    gn = gn_ref[...]
    st_scr[...] = s0_ref[...].T
    eye = (lax.broadcasted_iota(I32, (c, c), 0) == lax.broadcasted_iota(I32, (c, c), 1)).astype(F32)
    cm = cm_ref[...]

    def chunk(ci, carry):
        rows = pl.ds(pl.multiple_of(ci * c, c), c)
        xq = qr_ref[rows, :]
        q = xq * _sigmoid(xq)
        f = lb + (1.0 - lb) * _sigmoid(fr_ref[rows, :])
        k = 1.0 - f
        g = jnp.log(f)
        v = ir_ref[rows, :]
        if n_valid < t_total:
            live = (ci * c + lax.broadcasted_iota(I32, (c, 1), 0)) < n_valid
            k = jnp.where(live, k, 0.0)
            g = jnp.where(live, g, 0.0)
        vb = v.astype(BF16)

        g3 = jnp.concatenate(_split3(g), axis=-1)
        r3 = _dot(cm, g3)
        dk = g.shape[-1]
        ra = r3[:, :dk] + r3[:, dk:2 * dk] + r3[:, 2 * dk:]
        G = ra[:c]

        a = eye * jnp.sum(q * k, axis=-1, keepdims=True)
        small = 0
        for li, m in enumerate(levels):
            if m >= SUBLANES:
                pieces = [jnp.broadcast_to(G[b0 + m - 1:b0 + m, :], (2 * m, dk)) for b0 in range(0, c, 2 * m)]
                gref = pieces[0] if len(pieces) == 1 else jnp.concatenate(pieces, axis=0)
            else:
                small += 1
                gref = ra[small * c:(small + 1) * c]
            d = G - gref
            qe = (q * jnp.exp(jnp.minimum(d, 0.0))).astype(BF16)
            ke = (k * jnp.exp(jnp.minimum(-d, 0.0))).astype(BF16)
            a = a + _dot_nt(qe, ke) * mk_ref[li]

        st = st_scr[...]
        g_last = G[c - 1:c, :]
        o = _dot(a.astype(BF16), vb) + _dot_nt((q * jnp.exp(G)).astype(BF16), st.astype(BF16))
        kt = (k * jnp.exp(g_last - G)).astype(BF16)
        st_scr[...] = jnp.exp(g_last) * st + _dot_tn(vb, kt)

        o = o * lax.rsqrt(jnp.mean(o * o, axis=-1, keepdims=True) + RMS_EPS) * gn
        o = o * _sigmoid(og_ref[rows, :])
        o_ref[rows, :] = o.astype(o_ref.dtype)
        return carry

    lax.fori_loop(0, t_total // c, chunk, 0)
    sout_ref[...] = st_scr[...].T


def _hgrn(src, col0, t_rows, batch, out_rows, lb_logits, rec_norm_g, s0, layer, n_valid):
    nh, dh = N_REC_HEADS, REC_DIM
    cm, mk = _hgrn_consts(REC_CHUNK)
    cm = jnp.asarray(cm, BF16)
    mk = jnp.asarray(mk, F32)

    def seg_spec(j):
        return pl.BlockSpec((t_rows, dh), lambda b, h: (b, col0[j] + h))

    return pl.pallas_call(
        functools.partial(_hgrn_body, layer=layer, n_valid=n_valid),
        grid=(batch, nh),
        in_specs=[seg_spec(0), seg_spec(1), seg_spec(2), seg_spec(3),
                  pl.BlockSpec((lb_logits.shape[0], dh), lambda b, h: (0, h)),
                  pl.BlockSpec((1, dh), lambda b, h: (0, h)),
                  pl.BlockSpec((None, None, dh, dh), lambda b, h: (b, h, 0, 0)),
                  pl.BlockSpec(cm.shape, lambda b, h: (0, 0)),
                  pl.BlockSpec(mk.shape, lambda b, h: (0, 0, 0))],
        out_specs=[pl.BlockSpec((t_rows, dh), lambda b, h: (b, h)),
                   pl.BlockSpec((None, None, dh, dh), lambda b, h: (b, h, 0, 0))],
        out_shape=[jax.ShapeDtypeStruct((out_rows, nh * dh), BF16),
                   jax.ShapeDtypeStruct((batch, nh, dh, dh), F32)],
        scratch_shapes=[pltpu.VMEM((dh, dh), F32)],
        compiler_params=_cparams(2),
        name="hgrn2",
    )(src, src, src, src, lb_logits, rec_norm_g, s0, cm, mk)


def _merge_body(oa_ref, or_ref, oat_ref, ort_ref, ga0_ref, ga1_ref, gb0_ref, gb1_ref, x_ref, wpa_ref, wpb_ref,
                wout_ref, g1_ref, b1_ref, wr_ref, br_ref, x1_ref, route_ref, *, alpha, n_main):
    in_main = pl.program_id(0) < n_main
    a = _dot(jnp.where(in_main, oa_ref[...], oat_ref[...]), wpa_ref[...])
    b = _dot(jnp.where(in_main, or_ref[...], ort_ref[...]), wpb_ref[...])
    half = ga0_ref.shape[1]
    merged = jnp.concatenate(
        [_sigmoid(ga0_ref[...]) * a[:, :half] + _sigmoid(gb0_ref[...]) * b[:, :half],
         _sigmoid(ga1_ref[...]) * a[:, half:] + _sigmoid(gb1_ref[...]) * b[:, half:]], axis=-1)
    hmix = _dot(merged.astype(BF16), wout_ref[...])
    x1 = _layer_norm(alpha * x_ref[...] + hmix, g1_ref[...], b1_ref[...])
    x1_ref[...] = x1

    logits = _dot_nt_f32(x1, wr_ref[...]) + br_ref[...]
    lane = lax.broadcasted_iota(I32, logits.shape, 1)
    is_g = lane < N_GROUPS
    gl = jnp.where(is_g, logits, -jnp.inf)
    gmax = jnp.max(gl, axis=-1, keepdims=True)
    gidx = jnp.min(jnp.where(gl == gmax, lane, LANES), axis=-1, keepdims=True)
    g_w = 1.0 / jnp.sum(jnp.where(is_g, jnp.exp(gl - gmax), 0.0), axis=-1, keepdims=True)
    lo = N_GROUPS + gidx * EXPERTS_PER_GROUP
    el = jnp.where((lane >= lo) & (lane < lo + EXPERTS_PER_GROUP), logits, -jnp.inf)
    v1 = jnp.max(el, axis=-1, keepdims=True)
    i1 = jnp.min(jnp.where(el == v1, lane, LANES), axis=-1, keepdims=True)
    el2 = jnp.where(lane == i1, -jnp.inf, el)
    v2 = jnp.max(el2, axis=-1, keepdims=True)
    i2 = jnp.min(jnp.where(el2 == v2, lane, LANES), axis=-1, keepdims=True)
    e2 = jnp.exp(v2 - v1)
    w1 = g_w / (1.0 + e2)
    w2 = g_w * e2 / (1.0 + e2)
    route_ref[...] = jnp.where(lane == 0, (i1 - N_GROUPS).astype(F32),
                               jnp.where(lane == 1, (i2 - N_GROUPS).astype(F32),
                                         jnp.where(lane == 2, w1, jnp.where(lane == 3, w2, 0.0))))


def _merge(o_attn, o_rec, oa_tail, or_tail, proj, xcat, wpa, wpb, wout, g1, b1, wr, br, alpha, ga_blk, gb_blk, tm):
    mp, d = xcat.shape
    wa = o_attn.shape[1]
    half = d // 2
    n_main = o_attn.shape[0] // tm
    main = pl.BlockSpec((tm, wa), lambda i: (jnp.minimum(i, n_main - 1), 0))
    tail = pl.BlockSpec((tm, wa), lambda i: (jnp.maximum(i - n_main, 0), 0))

    def row(width, j=0):
        return pl.BlockSpec((tm, width), lambda i: (i, j))

    def whole(arr):
        return pl.BlockSpec(arr.shape, lambda i: (0,) * arr.ndim, pipeline_mode=pl.Buffered(1))

    return pl.pallas_call(
        functools.partial(_merge_body, alpha=alpha, n_main=n_main),
        grid=(mp // tm,),
        in_specs=[main, main, tail, tail, row(half, ga_blk), row(half, ga_blk + 1), row(half, gb_blk), row(half, gb_blk + 1),
                  row(d), whole(wpa), whole(wpb), whole(wout), whole(g1), whole(b1), whole(wr), whole(br)],
        out_specs=[row(d), row(LANES)],
        out_shape=[jax.ShapeDtypeStruct((mp, d), F32), jax.ShapeDtypeStruct((mp, LANES), F32)],
        compiler_params=_cparams(1),
        name="merge_ln_route",
    )(o_attn, o_rec, oa_tail, or_tail, proj, proj, proj, proj, xcat, wpa, wpb, wout, g1, b1, wr, br)


def _expert_body(be_ref, tok_ref, nused_ref, x_hbm, w1_ref, w3_ref, w2_ref, y_ref,
                 xbuf, sem, w1b, w3b, w2b):
    b = pl.program_id(0)
    bm = xbuf.shape[1]
    n_used = nused_ref[0]
    slot = b % 2

    def row_copy(blk, r, s):
        tok = tok_ref[blk * bm + r]
        return pltpu.make_async_copy(x_hbm.at[pl.ds(tok, 1)], xbuf.at[s, pl.ds(r, 1)], sem.at[s])

    def start_gather(blk, s):
        def issue(r, carry):
            row_copy(blk, r, s).start()
            return carry
        lax.fori_loop(0, bm, issue, 0)

    def wait_gather(blk, s):
        def wait(r, carry):
            row_copy(blk, r, s).wait()
            return carry
        lax.fori_loop(0, bm, wait, 0)

    @pl.when((b == 0) & (n_used > 0))
    def _():
        start_gather(0, 0)

    @pl.when(b + 1 < n_used)
    def _():
        start_gather(b + 1, 1 - slot)

    e = be_ref[b]
    prev = be_ref[jnp.maximum(b - 1, 0)]

    @pl.when((b < n_used) & ((b == 0) | (e != prev)))
    def _():
        w1b[...] = w1_ref[...].astype(BF16)
        w3b[...] = w3_ref[...].astype(BF16)
        w2b[...] = w2_ref[...].astype(BF16)

    @pl.when(b < n_used)
    def _():
        wait_gather(b, slot)
        xb = xbuf[slot].astype(BF16)
        h1 = _dot(xb, w1b[...])
        h3 = _dot(xb, w3b[...])
        hid = h1 * _sigmoid(h1) * h3
        y_ref[...] = _dot(hid.astype(BF16), w2b[...])

    @pl.when(b >= n_used)
    def _():
        y_ref[...] = jnp.zeros(y_ref.shape, y_ref.dtype)


def _experts(x1, w1, w3, w2, blk_expert, row_tok, n_used, bm):
    n_blk = blk_expert.shape[0]
    d = x1.shape[1]
    de = w1.shape[2]
    grid_spec = pltpu.PrefetchScalarGridSpec(
        num_scalar_prefetch=3,
        grid=(n_blk,),
        in_specs=[pl.BlockSpec(memory_space=pl.ANY),
                  pl.BlockSpec((None, d, de), lambda b, be, tok, nu: (be[b], 0, 0)),
                  pl.BlockSpec((None, d, de), lambda b, be, tok, nu: (be[b], 0, 0)),
                  pl.BlockSpec((None, de, d), lambda b, be, tok, nu: (be[b], 0, 0))],
        out_specs=pl.BlockSpec((bm, d), lambda b, be, tok, nu: (b, 0)),
        scratch_shapes=[pltpu.VMEM((2, bm, d), F32), pltpu.SemaphoreType.DMA((2,)),
                        pltpu.VMEM((d, de), BF16), pltpu.VMEM((d, de), BF16), pltpu.VMEM((de, d), BF16)],
    )
    return pl.pallas_call(
        _expert_body,
        grid_spec=grid_spec,
        out_shape=jax.ShapeDtypeStruct((n_blk * bm, d), F32),
        compiler_params=_cparams(1),
        name="moe_experts",
    )(blk_expert, row_tok, n_used, x1, w1, w3, w2)


def _combine_body(pos_ref, ys_hbm, x1_ref, route_ref, g2_ref, b2_ref, y_ref, ybuf, sem, *, alpha):
    i = pl.program_id(0)
    n = pl.num_programs(0)
    tm = x1_ref.shape[0]
    slot = i % 2

    def row_copy(tile, r, k, s):
        p = pos_ref[(tile * tm + r) * TOP_K + k]
        return pltpu.make_async_copy(ys_hbm.at[pl.ds(p, 1)], ybuf.at[s, k, pl.ds(r, 1)], sem.at[s])

    def start_gather(tile, s):
        def issue(r, carry):
            for k in range(TOP_K):
                row_copy(tile, r, k, s).start()
            return carry
        lax.fori_loop(0, tm, issue, 0)

    def wait_gather(tile, s):
        def wait(r, carry):
            for k in range(TOP_K):
                row_copy(tile, r, k, s).wait()
            return carry
        lax.fori_loop(0, tm, wait, 0)

    @pl.when(i == 0)
    def _():
        start_gather(0, 0)

    @pl.when(i + 1 < n)
    def _():
        start_gather(i + 1, 1 - slot)

    wait_gather(i, slot)
    route = route_ref[...]
    m = route[:, 2:3] * ybuf[slot, 0] + route[:, 3:4] * ybuf[slot, 1]
    y_ref[...] = _layer_norm(alpha * x1_ref[...] + m, g2_ref[...], b2_ref[...])


def _combine(y_sorted, pos, x1, route, g2, b2, alpha, tm):
    mp, d = x1.shape
    grid_spec = pltpu.PrefetchScalarGridSpec(
        num_scalar_prefetch=1,
        grid=(mp // tm,),
        in_specs=[pl.BlockSpec(memory_space=pl.ANY),
                  pl.BlockSpec((tm, d), lambda i, pos: (i, 0)),
                  pl.BlockSpec((tm, LANES), lambda i, pos: (i, 0)),
                  pl.BlockSpec((1, d), lambda i, pos: (0, 0)),
                  pl.BlockSpec((1, d), lambda i, pos: (0, 0))],
        out_specs=pl.BlockSpec((tm, d), lambda i, pos: (i, 0)),
        scratch_shapes=[pltpu.VMEM((2, TOP_K, tm, d), F32), pltpu.SemaphoreType.DMA((2,))],
    )
    return pl.pallas_call(
        functools.partial(_combine_body, alpha=alpha),
        grid_spec=grid_spec,
        out_shape=jax.ShapeDtypeStruct((mp, d), F32),
        compiler_params=_cparams(1),
        name="moe_combine_ln",
    )(pos, y_sorted, x1, route, g2, b2)


def _dispatch_plan(expert, n_tok_pad, bm):
    n_real = expert.shape[0]
    flat_e = expert.reshape(-1)
    nk = flat_e.shape[0]
    onehot = (flat_e[:, None] == jnp.arange(N_EXPERTS, dtype=I32)[None, :]).astype(I32)
    csum = jnp.cumsum(onehot, axis=0)
    counts = csum[-1]
    rank = jnp.sum(csum * onehot, axis=1) - 1
    padded = (counts + bm - 1) // bm * bm
    pend = jnp.cumsum(padded)
    pstart = pend - padded
    dest = (pstart[flat_e] + rank).astype(I32)
    n_blk = -(-nk // bm) + N_EXPERTS
    row_tok = jnp.zeros((n_blk * bm,), I32).at[dest].set((jnp.arange(nk, dtype=I32) // TOP_K))
    blk_start = jnp.arange(n_blk, dtype=I32) * bm
    blk_expert = jnp.minimum(jnp.sum(pend[None, :] <= blk_start[:, None], axis=1), N_EXPERTS - 1).astype(I32)
    n_used = (pend[-1] // bm).astype(I32).reshape(1)
    pos = jnp.zeros((n_tok_pad * TOP_K,), I32).at[:nk].set(dest)
    del n_real
    return row_tok, blk_expert, n_used, pos


def _alibi_slopes():
    return (2.0 ** (-8.0 * np.arange(1, N_ATTN_HEADS + 1) / N_ATTN_HEADS)).astype(np.float32)


def _decode_consts(page_tokens):
    nh = N_ATTN_HEADS
    r = nh * SUBLANES
    p = page_tokens * nh
    row_h = np.arange(r) // SUBLANES
    row_t = np.arange(r) % SUBLANES
    lane_tok = np.arange(p) // nh
    lane_h = np.arange(p) % nh
    slopes = _alibi_slopes()
    nsl = (-slopes[row_h]).astype(np.float32)[:, None]
    hmask = (row_h[:, None] == lane_h[None, :]).astype(np.float32)
    bias = np.stack([nsl * (row_t[:, None] - (lane_tok[None, :] + half * page_tokens)) for half in range(2)])
    return nsl, hmask, bias.astype(np.float32)


def _layer(xcat, xb, n_prompt, batch, seq, dec_batch, dec_seq, cache_k_l, cache_v_l, state_l, page_table,
           w_in_l, lb_logits, layer, rec_norm_g_l, w_pa_l, w_pb_l, w_out_l, ln1_g_l, ln1_b_l,
           w_group_l, b_group_l, w_router_l, b_router_l, w1_l, w3_l, w2_l, ln2_g_l, ln2_b_l, alpha):
    mp, d = xcat.shape
    nh, hd = N_ATTN_HEADS, ATTN_HEAD_DIM
    aw = nh * hd
    rw = N_REC_HEADS * REC_DIM
    n_real = n_prompt + dec_batch * dec_seq
    slopes = _alibi_slopes()

    proj = _in_proj(xb, w_in_l, INPROJ_ROWS if mp % INPROJ_ROWS == 0 else ROW_TILE, SEG)
    seg_blk = SEG // hd

    o_attn = _moba_prompt(proj, jnp.asarray(slopes), batch, seq, n_prompt)
    ps = proj[n_prompt:n_real]
    page_tokens = cache_k_l.shape[1]
    past_len = page_table.shape[1] * page_tokens
    r = nh * SUBLANES

    def to_rows(t):
        t = t.reshape(dec_batch, dec_seq, nh, hd).transpose(0, 2, 1, 3)
        return jnp.pad(t, ((0, 0), (0, 0), (0, SUBLANES - dec_seq), (0, 0)))

    q_s = ps[:, 0:aw] * (hd ** -0.5)
    k_s = ps[:, aw:2 * aw]
    v_s = ps[:, 2 * aw:3 * aw]
    q_rows = to_rows(q_s).reshape(dec_batch, r, hd)

    def new_rows(t):
        t = t.reshape(dec_batch, dec_seq, nh, 1, hd)
        return jnp.broadcast_to(t, (dec_batch, dec_seq, nh, SUBLANES, hd)).reshape(dec_batch, dec_seq, r, hd)

    nsl, hmask, bias = _decode_consts(page_tokens)
    ck2 = cache_k_l.reshape(cache_k_l.shape[0], page_tokens * nh, hd)
    cv2 = cache_v_l.reshape(cache_v_l.shape[0], page_tokens * nh, hd)
    stats, o_blk = _moba_decode_stats(q_rows.astype(BF16), ck2, cv2, page_table, jnp.asarray(bias),
                                      jnp.asarray(hmask), jnp.asarray(nsl), past_len)
    o_s = _moba_decode_merge(stats, o_blk, q_rows, new_rows(k_s), new_rows(v_s), jnp.asarray(nsl), dec_seq)
    o_s = o_s.reshape(dec_batch, nh, SUBLANES, hd)[:, :, :dec_seq].transpose(0, 2, 1, 3).reshape(dec_batch * dec_seq, aw)

    zeros_state = jnp.zeros((batch,) + state_l.shape[1:], F32)
    cols = tuple((3 + j) * seg_blk for j in range(4))
    o_rec, rec_p = _hgrn(proj, cols, seq, batch, n_prompt, lb_logits, rec_norm_g_l, zeros_state, layer, seq)
    t_pad = REC_CHUNK
    hs = ps[:, 3 * aw:3 * aw + 4 * rw].reshape(dec_batch, dec_seq, 4 * rw)
    hs = jnp.pad(hs, ((0, 0), (0, t_pad - dec_seq), (0, 0))).reshape(dec_batch * t_pad, 4 * rw)
    cols_s = tuple(j * seg_blk for j in range(4))
    o_rec_s, rec_s = _hgrn(hs, cols_s, t_pad, dec_batch, dec_batch * t_pad, lb_logits, rec_norm_g_l,
                           state_l.astype(F32), layer, dec_seq)
    o_rec_s = o_rec_s.reshape(dec_batch, t_pad, rw)[:, :dec_seq].reshape(dec_batch * dec_seq, rw)

    tail = jnp.zeros((mp - n_prompt, aw), BF16)
    oa_tail = tail.at[:n_real - n_prompt].set(o_s.astype(BF16))
    or_tail = tail.at[:n_real - n_prompt].set(o_rec_s)

    ne = N_GROUPS + N_EXPERTS
    wr = jnp.concatenate([w_group_l.T, w_router_l.transpose(0, 2, 1).reshape(N_EXPERTS, d)], axis=0)
    wr = jnp.pad(wr, ((0, LANES - ne), (0, 0)))
    br = jnp.pad(jnp.concatenate([b_group_l, b_router_l.reshape(-1)]), (0, LANES - ne)).reshape(1, LANES)
    ga_blk = (3 * aw + 4 * rw) // (d // 2)
    gb_blk = ga_blk + 2
    x1, route = _merge(o_attn, o_rec, oa_tail, or_tail, proj, xcat, w_pa_l.astype(BF16), w_pb_l.astype(BF16), w_out_l.astype(BF16),
                       ln1_g_l.reshape(1, d), ln1_b_l.reshape(1, d), wr, br, alpha, ga_blk, gb_blk, ROW_TILE)

    expert = route[:n_real, 0:TOP_K].astype(I32)
    row_tok, blk_expert, n_used, pos = _dispatch_plan(expert, mp, MOE_ROWS)
    y_sorted = _experts(x1, w1_l, w3_l, w2_l, blk_expert, row_tok, n_used, MOE_ROWS)
    y = _combine(y_sorted, pos, x1, route, ln2_g_l.reshape(1, d), ln2_b_l.reshape(1, d), alpha, ROW_TILE)

    k_all = proj[:n_real, aw:2 * aw]
    v_all = proj[:n_real, 2 * aw:3 * aw]
    return y, k_all, v_all, rec_p, rec_s


def kernel(x_prompt, x_sample, cache_k, cache_v, state_rec, page_table, w_in, lb_logits, rec_norm_g, w_pa, w_pb,
           w_out, ln1_g, ln1_b, w_group, b_group, w_router, b_router, w1, w3, w2, ln2_g, ln2_b):
    depth = w_in.shape[0]
    batch, seq, d = x_prompt.shape
    dec_batch, dec_seq, _ = x_sample.shape
    nh, hd = N_ATTN_HEADS, ATTN_HEAD_DIM
    alpha = (2 * depth) ** 0.25
    n_prompt = batch * seq
    n_real = n_prompt + dec_batch * dec_seq
    mp = -(-n_real // ROW_TILE) * ROW_TILE
    xcat = jnp.concatenate([x_prompt.reshape(n_prompt, d), x_sample.reshape(dec_batch * dec_seq, d),
                            jnp.zeros((mp - n_real, d), x_prompt.dtype)], axis=0)

    kp, vp, sp, ks, vs, ss = [], [], [], [], [], []
    for l in range(depth):
        y, k_all, v_all, rec_p, rec_s = _layer(
            xcat, xcat.astype(BF16), n_prompt, batch, seq, dec_batch, dec_seq, cache_k[l], cache_v[l],
            state_rec[l], page_table, w_in[l], lb_logits, l, rec_norm_g[l].reshape(1, -1), w_pa[l], w_pb[l],
            w_out[l], ln1_g[l], ln1_b[l], w_group[l], b_group[l], w_router[l], b_router[l], w1[l], w3[l], w2[l],
            ln2_g[l], ln2_b[l], alpha)
        xcat = y
        kp.append(k_all[:n_prompt].reshape(batch, seq, nh, hd))
        vp.append(v_all[:n_prompt].reshape(batch, seq, nh, hd))
        ks.append(k_all[n_prompt:].reshape(dec_batch, dec_seq, nh, hd))
        vs.append(v_all[n_prompt:].reshape(dec_batch, dec_seq, nh, hd))
        sp.append(rec_p.astype(state_rec.dtype))
        ss.append(rec_s.astype(state_rec.dtype))
    y_prompt = xcat[:n_prompt].reshape(batch, seq, d)
    y_sample = xcat[n_prompt:n_real].reshape(dec_batch, dec_seq, d)
    return (y_prompt, y_sample, jnp.stack(kp), jnp.stack(vp), jnp.stack(sp), jnp.stack(ks), jnp.stack(vs),
            jnp.stack(ss))
```
